```python
import math
import jax, jax.numpy as jnp
from jax import lax
import numpy as np

D_MODEL = 1024
BATCH = 8
SEQ = 2048
DEPTH = 2

N_META = 16
DN_HEADS = 4
DN_DK = 128
DN_DV = 128
DN_KEY = DN_HEADS * DN_DK
DN_VAL = DN_HEADS * DN_DV
CONV_W = 4
CHUNK = 64
SA_HEADS = 8
SA_DH = 64
SA_W = SA_HEADS * SA_DH
IDX_HEADS = 8
IDX_DH = 64
K_MAX = 256
Q_BLOCK = 128
ROPE_THETA = 10000.0
MIX_W = DN_VAL + SA_W
N_GROUPS = 4
EXP_PER_GROUP = 8
N_EXPERTS = N_GROUPS * EXP_PER_GROUP
TOP_K_IN_GROUP = 2
D_EXPERT = 256
DEEPNORM_ALPHA = (2.0 * DEPTH) ** 0.25
DEEPNORM_BETA = (8.0 * DEPTH) ** -0.25

SPLIT_SIZES = (DN_KEY, DN_KEY, DN_VAL, DN_VAL, DN_HEADS, DN_HEADS,
               SA_W, SA_W, SA_W, IDX_HEADS * IDX_DH, IDX_DH, IDX_HEADS)
SPLIT_POINTS = tuple(int(s) for s in np.cumsum(SPLIT_SIZES)[:-1])
D_IN_PROJ = int(sum(SPLIT_SIZES))
VALUE_COLS = (2, 8)

kernel_name = 'hybrid_deltanet_dsa_hmoe'

F32 = jnp.float32


def layer_norm(x, g, b, eps=1e-5):
    xf = x.astype(F32)
    mu = jnp.mean(xf, -1, keepdims=True)
    var = jnp.mean(jnp.square(xf - mu), -1, keepdims=True)
    return ((xf - mu) * lax.rsqrt(var + eps) * g + b).astype(x.dtype)


def rope_tables(n_pos, dim):
    inv = 1.0 / (ROPE_THETA ** (jnp.arange(0, dim, 2, dtype=F32) / dim))
    ang = jnp.arange(n_pos, dtype=F32)[:, None] * inv[None, :]
    ang = jnp.concatenate([ang, ang], -1)
    return jnp.cos(ang), jnp.sin(ang)


def apply_rope(x, cos, sin):
    half = x.shape[-1] // 2
    rot = jnp.concatenate([-x[..., half:], x[..., :half]], -1)
    shape = (1, cos.shape[0]) + (1,) * (x.ndim - 3) + (cos.shape[-1],)
    return (x * cos.reshape(shape) + rot * sin.reshape(shape)).astype(x.dtype)


def causal_dwconv(x, w):
    c = x.shape[-1]
    return lax.conv_general_dilated(x, w[:, None, :].astype(x.dtype), window_strides=(1,),
                                    padding=[(CONV_W - 1, 0)],
                                    dimension_numbers=('NWC', 'WIO', 'NWC'),
                                    feature_group_count=c)


def l2norm(x, eps=1e-6):
    return x * lax.rsqrt(jnp.sum(x * x, -1, keepdims=True) + eps)


def chunk_gated_delta_rule(q, k, v, g, beta):
    bsz, seq, h, dk = q.shape
    dv = v.shape[-1]
    n = seq // CHUNK

    def chunks(t):
        t = t.reshape((bsz, n, CHUNK, h) + t.shape[3:])
        return jnp.moveaxis(t, 3, 2)

    q, k, v, g, beta = (chunks(t) for t in (q, k, v, g, beta))
    q = q * dk ** -0.5
    gc = jnp.cumsum(g, axis=-1)
    pos = jnp.arange(CHUNK)
    tril = pos[:, None] >= pos[None, :]
    strict = pos[:, None] > pos[None, :]
    decay = jnp.exp(jnp.where(tril, gc[..., :, None] - gc[..., None, :], -jnp.inf))
    kb = k * beta[..., None]
    lower = jnp.where(strict, jnp.einsum('bnhid,bnhjd->bnhij', kb, k) * decay, 0.0)
    eye = jnp.eye(CHUNK, dtype=q.dtype)
    tmat = lax.linalg.triangular_solve(eye + lower, jnp.broadcast_to(eye, lower.shape),
                                       left_side=True, lower=True, unit_diagonal=True)
    u = tmat @ (v * beta[..., None])
    w = tmat @ (kb * jnp.exp(gc)[..., None])
    q_dec = q * jnp.exp(gc)[..., None]
    a_qk = jnp.where(tril, jnp.einsum('bnhid,bnhjd->bnhij', q, k) * decay, 0.0)
    g_last = gc[..., -1]
    k_dec = k * jnp.exp(g_last[..., None] - gc)[..., None]

    def step(state, inp):
        u_n, w_n, qd_n, a_n, kd_n, gl_n = inp
        v_new = u_n - jnp.einsum('bhck,bhkv->bhcv', w_n, state)
        o_n = jnp.einsum('bhck,bhkv->bhcv', qd_n, state) + jnp.einsum('bhij,bhjv->bhiv', a_n, v_new)
        state = state * jnp.exp(gl_n)[..., None, None] + jnp.einsum('bhck,bhcv->bhkv', kd_n, v_new)
        return state, o_n

    xs = tuple(jnp.moveaxis(t, 1, 0) for t in (u, w, q_dec, a_qk, k_dec, g_last))
    s0 = jnp.zeros((bsz, h, dk, dv), q.dtype)
    _, o = lax.scan(step, s0, xs)
    o = jnp.moveaxis(o, 0, 1)
    return jnp.moveaxis(o, 3, 2).reshape(bsz, seq, h, dv)


def deltanet_group(q, k, v, z, b_logit, a_logit, conv_w, a_log, dt_bias, norm_g):
    bsz, seq, _ = q.shape
    qkv = jax.nn.silu(causal_dwconv(jnp.concatenate([q, k, v], -1), conv_w)).astype(F32)
    q, k, v = jnp.split(qkv, [DN_KEY, 2 * DN_KEY], axis=-1)
    q = l2norm(q.reshape(bsz, seq, DN_HEADS, DN_DK))
    k = l2norm(k.reshape(bsz, seq, DN_HEADS, DN_DK))
    v = v.reshape(bsz, seq, DN_HEADS, DN_DV)
    beta = jax.nn.sigmoid(b_logit.astype(F32))
    g = -jnp.exp(a_log.astype(F32)) * jax.nn.softplus(a_logit.astype(F32) + dt_bias.astype(F32))
    pad = (-N_META) % CHUNK
    padf = lambda t: jnp.pad(t, ((0, 0), (pad, 0)) + ((0, 0),) * (t.ndim - 2))
    o = chunk_gated_delta_rule(padf(q), padf(k), padf(v), padf(g), padf(beta))[:, pad:]
    o = o * lax.rsqrt(jnp.mean(o * o, -1, keepdims=True) + 1e-6) * norm_g
    o = o.reshape(bsz, seq, DN_VAL) * jax.nn.silu(z.astype(F32))
    return o.astype(z.dtype)


def dsa_group(q, k, v, qi, ki, wi, cos, sin, cos_i, sin_i, k_top):
    bsz, seq, _ = q.shape
    q = apply_rope(q.reshape(bsz, seq, SA_HEADS, SA_DH), cos, sin)
    k = apply_rope(k.reshape(bsz, seq, SA_HEADS, SA_DH), cos, sin)
    v = v.reshape(bsz, seq, SA_HEADS, SA_DH)
    qi = apply_rope(qi.reshape(bsz, seq, IDX_HEADS, IDX_DH), cos_i, sin_i)
    ki = apply_rope(ki, cos_i, sin_i)
    wi = wi.astype(F32) * IDX_HEADS ** -0.5
    n_blk = -(-seq // Q_BLOCK)
    lq = n_blk * Q_BLOCK

    def blocks(t):
        t = jnp.pad(t, ((0, 0), (0, lq - seq)) + ((0, 0),) * (t.ndim - 2))
        return jnp.moveaxis(t.reshape((bsz, n_blk, Q_BLOCK) + t.shape[2:]), 1, 0)

    key_pos = jnp.arange(seq)
    q_pos = jnp.arange(lq).reshape(n_blk, Q_BLOCK)
    gather = jax.vmap(lambda arr, idx: arr[idx])

    def attend(args):
        qb, qib, wib, pos = args
        causal = key_pos[None, :] <= pos[:, None]
        rel = jax.nn.relu(jnp.einsum('bqhd,bsd->bqhs', qib, ki, preferred_element_type=F32) * IDX_DH ** -0.5)
        score = jnp.einsum('bqhs,bqh->bqs', rel, wib)
        score = jnp.where(causal[None], score, -jnp.inf)
        _, sel = lax.top_k(score, k_top)
        k_sel = gather(k, sel)
        v_sel = gather(v, sel)
        valid = sel <= pos[None, :, None]
        logits = jnp.einsum('bqhd,bqkhd->bhqk', qb, k_sel, preferred_element_type=F32) * SA_DH ** -0.5
        logits = jnp.where(valid[:, None], logits, -jnp.inf)
        p = jax.nn.softmax(logits, axis=-1)
        return jnp.einsum('bhqk,bqkhd->bqhd', p.astype(v.dtype), v_sel)

    out = lax.map(attend, (blocks(q), blocks(qi), blocks(wi), q_pos))
    return jnp.moveaxis(out, 0, 1).reshape(bsz, lq, SA_W)[:, :seq]


def hier_moe(h, w_grp, b_grp, w_rtr, b_rtr, w1, w3, w2):
    bsz, seq, d = h.shape
    t = h.reshape(-1, d)
    grp_prob = jax.nn.softmax((t @ w_grp).astype(F32) + b_grp, axis=-1)
    g_prob, g_sel = lax.top_k(grp_prob, 1)
    exp_logits = ((t @ w_rtr).astype(F32) + b_rtr).reshape(-1, N_GROUPS, EXP_PER_GROUP)
    in_grp = jnp.take_along_axis(exp_logits, g_sel[:, :, None], axis=1)[:, 0]
    e_prob, e_sel = lax.top_k(jax.nn.softmax(in_grp, axis=-1), TOP_K_IN_GROUP)
    gate = g_prob * e_prob / jnp.sum(e_prob, -1, keepdims=True)
    expert_id = g_sel * EXP_PER_GROUP + e_sel
    combine = jnp.sum(jax.nn.one_hot(expert_id, N_EXPERTS, dtype=F32) * gate[..., None], axis=1)
    hid = jax.nn.silu(jnp.einsum('td,edf->tef', t, w1)) * jnp.einsum('td,edf->tef', t, w3)
    y = jnp.einsum('tef,efd->td', hid * combine[..., None].astype(hid.dtype), w2)
    return y.reshape(bsz, seq, d)


def setup_inputs(seed: int = 0) -> dict:
    key = jax.random.key(seed)
    ks = jax.random.split(key, 20)
    nrm = lambda k, shape, s: jax.random.normal(k, shape, F32) * s
    d_sc = D_MODEL ** -0.5
    col_scale = np.concatenate([np.full((s,), DEEPNORM_BETA if i in VALUE_COLS else 1.0, np.float32)
                                for i, s in enumerate(SPLIT_SIZES)])
    dt = jnp.exp(jax.random.uniform(ks[5], (DEPTH, DN_HEADS), F32, math.log(1e-3), math.log(1e-1)))
    return {
        'x': nrm(ks[0], (BATCH, SEQ, D_MODEL), 1.0),
        'meta_tokens': nrm(ks[1], (N_META, D_MODEL), 1.0),
        'w_in': nrm(ks[2], (DEPTH, D_MODEL, D_IN_PROJ), d_sc) * jnp.asarray(col_scale),
        'conv_w': nrm(ks[3], (DEPTH, CONV_W, 2 * DN_KEY + DN_VAL), CONV_W ** -0.5),
        'a_log': jnp.log(jax.random.uniform(ks[4], (DEPTH, DN_HEADS), F32, 1.0, 16.0)),
        'dt_bias': dt + jnp.log(-jnp.expm1(-dt)),
        'dn_norm_g': 1.0 + nrm(ks[6], (DEPTH, DN_DV), 0.02),
        'w_out': nrm(ks[7], (DEPTH, MIX_W, D_MODEL), MIX_W ** -0.5 * DEEPNORM_BETA),
        'ln1_g': 1.0 + nrm(ks[8], (DEPTH, D_MODEL), 0.02),
        'ln1_b': nrm(ks[9], (DEPTH, D_MODEL), 0.02),
        'w_grp': nrm(ks[10], (DEPTH, D_MODEL, N_GROUPS), d_sc),
        'b_grp': nrm(ks[11], (DEPTH, N_GROUPS), 0.01),
        'w_rtr': nrm(ks[12], (DEPTH, D_MODEL, N_EXPERTS), d_sc),
        'b_rtr': nrm(ks[13], (DEPTH, N_EXPERTS), 0.01),
        'w1': nrm(ks[14], (DEPTH, N_EXPERTS, D_MODEL, D_EXPERT), d_sc),
        'w3': nrm(ks[15], (DEPTH, N_EXPERTS, D_MODEL, D_EXPERT), d_sc),
        'w2': nrm(ks[16], (DEPTH, N_EXPERTS, D_EXPERT, D_MODEL), D_EXPERT ** -0.5 * DEEPNORM_BETA),
        'ln2_g': 1.0 + nrm(ks[17], (DEPTH, D_MODEL), 0.02),
        'ln2_b': nrm(ks[18], (DEPTH, D_MODEL), 0.02),
    }


def reference(x, meta_tokens, w_in, conv_w, a_log, dt_bias, dn_norm_g, w_out, ln1_g, ln1_b,
              w_grp, b_grp, w_rtr, b_rtr, w1, w3, w2, ln2_g, ln2_b):
    bsz, seq, _ = x.shape
    total = seq + N_META
    k_top = min(K_MAX, seq // 4)
    meta = jnp.broadcast_to(meta_tokens[None].astype(x.dtype), (bsz, N_META, D_MODEL))
    h = jnp.concatenate([meta, x], axis=1)
    cos, sin = rope_tables(total, SA_DH)
    cos_i, sin_i = rope_tables(total, IDX_DH)
    for l in range(DEPTH):
        proj = h @ w_in[l]
        dq, dk, dv, dz, db, da, sq, sk, sv, iq, ik, iw = jnp.split(proj, SPLIT_POINTS, axis=-1)
        o_dn = deltanet_group(dq, dk, dv, dz, db, da, conv_w[l], a_log[l], dt_bias[l], dn_norm_g[l])
        o_sa = dsa_group(sq, sk, sv, iq, ik, iw, cos, sin, cos_i, sin_i, k_top)
        mix = jnp.concatenate([o_dn, o_sa.astype(o_dn.dtype)], axis=-1) @ w_out[l]
        h = layer_norm(DEEPNORM_ALPHA * h + mix, ln1_g[l], ln1_b[l])
        ffn = hier_moe(h, w_grp[l], b_grp[l], w_rtr[l], b_rtr[l], w1[l], w3[l], w2[l])
        h = layer_norm(DEEPNORM_ALPHA * h + ffn, ln2_g[l], ln2_b[l])
    return h[:, N_META:]
```

```python
import functools

import jax
import jax.numpy as jnp
import numpy as np
from jax import lax
from jax.experimental import pallas as pl
from jax.experimental.pallas import tpu as pltpu

F32 = jnp.float32
BF16 = jnp.bfloat16
HIGHEST = lax.Precision.HIGHEST

N_META = 16
DN_HEADS = 4
DN_DK = 128
DN_DV = 128
CONV_W = 4
CHUNK = 64
SA_HEADS = 8
SA_DH = 64
IDX_HEADS = 8
IDX_DH = 64
K_MAX = 256
ROPE_THETA = 10000.0
N_GROUPS = 4
EXP_PER_GROUP = 8
N_EXPERTS = N_GROUPS * EXP_PER_GROUP
D_EXPERT = 256
DEPTH = 2
DEEPNORM_ALPHA = (2.0 * DEPTH) ** 0.25

LANE = 128
QBLK = 128
VMEM_LIMIT = 52 * 1024 * 1024

C_DNQ, C_DNK, C_DNV, C_DNZ = 0, 512, 1024, 1536
C_SAQ, C_SAK, C_IDQ, C_SMALL = 2048, 2560, 3072, 3584
N_PROJ = 3712
SM_IDK, SM_BETA, SM_DECAY, SM_IDW = 0, 64, 68, 72


def _nt(a, b, **kw):
    return lax.dot_general(a, b, (((1,), (1,)), ((), ())), preferred_element_type=F32, **kw)


def _tn(a, b, **kw):
    return lax.dot_general(a, b, (((0,), (0,)), ((), ())), preferred_element_type=F32, **kw)


def _mm(a, b, **kw):
    return jnp.dot(a, b, preferred_element_type=F32, **kw)


def _sigmoid(x):
    return 1.0 / (1.0 + jnp.exp(-x))


def _silu(x):
    return x * _sigmoid(x)


def _params(*sem):
    return pltpu.CompilerParams(dimension_semantics=sem, vmem_limit_bytes=VMEM_LIMIT)


def _proj_kernel(x_ref, w_ref, o_ref):
    o_ref[...] = _mm(x_ref[...], w_ref[...])


def _in_proj(xb, w, tm):
    m, k = xb.shape
    n = w.shape[1]
    return pl.pallas_call(
        _proj_kernel,
        out_shape=jax.ShapeDtypeStruct((m, n), F32),
        grid=(m // tm,),
        in_specs=[pl.BlockSpec((tm, k), lambda i: (i, 0)),
                  pl.BlockSpec((k, n), lambda i: (0, 0))],
        out_specs=pl.BlockSpec((tm, n), lambda i: (i, 0)),
        compiler_params=_params("parallel"),
        name="in_proj",
    )(xb, w)


def _vt_kernel(w_ref, x_ref, o_ref):
    o_ref[...] = _nt(w_ref[...], x_ref[...]).astype(o_ref.dtype)


def _v_transposed(xb, wv_t, bsz, lb):
    n, k = wv_t.shape
    return pl.pallas_call(
        _vt_kernel,
        out_shape=jax.ShapeDtypeStruct((bsz, n, lb), BF16),
        grid=(bsz,),
        in_specs=[pl.BlockSpec((n, k), lambda b: (0, 0)),
                  pl.BlockSpec((lb, k), lambda b: (b, 0))],
        out_specs=pl.BlockSpec((None, n, lb), lambda b: (b, 0, 0)),
        compiler_params=_params("parallel"),
        name="v_transposed",
    )(wv_t, xb)


def _dn_kernel(hp_ref, q_ref, k_ref, v_ref, z_ref, sm_ref, cwq_ref, cwk_ref, cwv_ref, ng_ref,
               o_ref, qs, ks, vs, gs, bs, *, front, total):
    lb = q_ref.shape[0]
    head = pl.program_id(1)
    rows = lax.broadcasted_iota(jnp.int32, (lb, 1), 0)
    valid = (rows >= front) & (rows < front + total)

    def conv_silu(x_ref, cw_ref):
        x = jnp.where(valid, x_ref[...], 0.0)
        cw = cw_ref[...]
        acc = x * cw[CONV_W - 1:CONV_W, :]
        for j in range(1, CONV_W):
            acc = acc + pltpu.roll(x, j, 0) * cw[CONV_W - 1 - j:CONV_W - j, :]
        return _silu(acc)

    def l2n(x):
        return x * lax.rsqrt(jnp.sum(x * x, axis=-1, keepdims=True) + 1e-6)

    qs[...] = l2n(conv_silu(q_ref, cwq_ref)) * (DN_DK ** -0.5)
    ks[...] = l2n(conv_silu(k_ref, cwk_ref))
    vs[...] = conv_silu(v_ref, cwv_ref)

    lane = lax.broadcasted_iota(jnp.int32, (1, LANE), 1)
    hp = hp_ref[...]
    a_log = jnp.sum(jnp.where(lane == head, hp[0:1, :], 0.0), axis=-1, keepdims=True)
    dt_b = jnp.sum(jnp.where(lane == head, hp[1:2, :], 0.0), axis=-1, keepdims=True)
    sm = sm_ref[...]
    b_logit = jnp.sum(jnp.where(lane == SM_BETA + head, sm, 0.0), axis=-1, keepdims=True)
    a_logit = jnp.sum(jnp.where(lane == SM_DECAY + head, sm, 0.0), axis=-1, keepdims=True)
    xg = a_logit + dt_b
    softplus = jnp.maximum(xg, 0.0) + jnp.log1p(jnp.exp(-jnp.abs(xg)))
    g = -jnp.exp(a_log) * softplus
    gs[...] = jnp.broadcast_to(jnp.where(valid, g, 0.0), (lb, LANE))
    bs[...] = jnp.broadcast_to(jnp.where(valid, _sigmoid(b_logit), 0.0), (lb, LANE))

    ri = lax.broadcasted_iota(jnp.int32, (CHUNK, CHUNK), 0)
    ci = lax.broadcasted_iota(jnp.int32, (CHUNK, CHUNK), 1)
    tril = ri >= ci
    strict = ri > ci
    tril_f = tril.astype(F32)
    eye = (ri == ci).astype(F32)
    lane0 = (lax.broadcasted_iota(jnp.int32, (CHUNK, LANE), 1) == 0).astype(F32)
    ng = ng_ref[...]
    n_chunks = (front + total) // CHUNK

    def chunk(c, state):
        r0 = pl.multiple_of(c * CHUNK, CHUNK)
        sl = pl.ds(r0, CHUNK)
        q, k, v, gb, bb = qs[sl, :], ks[sl, :], vs[sl, :], gs[sl, :], bs[sl, :]
        gcb = _mm(tril_f, gb, precision=HIGHEST)
        gc_row = _nt(lane0, gcb, precision=HIGHEST)
        diff = gcb[:, :CHUNK] - gc_row
        decay = jnp.where(tril, jnp.exp(jnp.where(tril, diff, 0.0)), 0.0)
        eg = jnp.exp(gcb)
        kb = k * bb
        kbf = k.astype(BF16)
        lower = jnp.where(strict, _nt(kb.astype(BF16), kbf) * decay, 0.0)
        a_qk = jnp.where(tril, _nt(q.astype(BF16), kbf) * decay, 0.0)
        x = eye - lower
        p = lower
        for _ in range(5):
            p = _mm(p, p, precision=HIGHEST)
            x = x + _mm(x, p, precision=HIGHEST)
        xb16 = x.astype(BF16)
        u = _mm(xb16, (v * bb).astype(BF16))
        w = _mm(xb16, (kb * eg).astype(BF16))
        g_last = gcb[CHUNK - 1:CHUNK, :]
        kd = k * jnp.exp(g_last - gcb)
        sb = state.astype(BF16)
        v_new = u - _mm(w.astype(BF16), sb)
        o = _mm((q * eg).astype(BF16), sb) + _mm(a_qk.astype(BF16), v_new.astype(BF16))
        state = state * jnp.exp(g_last) + _tn(kd.astype(BF16), v_new.astype(BF16))
        o = o * lax.rsqrt(jnp.mean(o * o, axis=-1, keepdims=True) + 1e-6) * ng
        o_ref[sl, :] = (o * _silu(z_ref[sl, :])).astype(o_ref.dtype)
        return state

    lax.fori_loop(0, n_chunks, chunk, jnp.zeros((DN_DK, DN_DV), F32))
    tail = lb - n_chunks * CHUNK
    if tail:
        o_ref[pl.ds(n_chunks * CHUNK, tail), :] = jnp.zeros((tail, DN_DV), o_ref.dtype)


def _deltanet(proj, hp, conv_w, norm_g, bsz, lb, front, total):
    cb = lambda base: (lambda b, h: (b, base // LANE + h))
    cwb = lambda base: (lambda b, h: (0, base // LANE + h))
    blk = lambda f: pl.BlockSpec((lb, LANE), f)
    return pl.pallas_call(
        functools.partial(_dn_kernel, front=front, total=total),
        out_shape=jax.ShapeDtypeStruct((bsz * lb, DN_HEADS * DN_DV), BF16),
        grid=(bsz, DN_HEADS),
        in_specs=[pl.BlockSpec((8, LANE), lambda b, h: (0, 0)),
                  blk(cb(C_DNQ)), blk(cb(C_DNK)), blk(cb(C_DNV)), blk(cb(C_DNZ)),
                  blk(lambda b, h: (b, C_SMALL // LANE)),
                  pl.BlockSpec((CONV_W, LANE), cwb(C_DNQ)),
                  pl.BlockSpec((CONV_W, LANE), cwb(C_DNK)),
                  pl.BlockSpec((CONV_W, LANE), cwb(C_DNV)),
                  pl.BlockSpec((1, LANE), lambda b, h: (0, 0))],
        out_specs=blk(lambda b, h: (b, h)),
        scratch_shapes=[pltpu.VMEM((lb, LANE), F32) for _ in range(5)],
        compiler_params=_params("parallel", "arbitrary"),
        name="deltanet",
    )(hp, proj, proj, proj, proj, proj, conv_w, conv_w, conv_w, norm_g)


def _rope(x, cos2, sin2):
    lane = lax.broadcasted_iota(jnp.int32, x.shape, 1)
    fwd = pltpu.roll(x, SA_DH // 2, 1)
    bwd = pltpu.roll(x, LANE - SA_DH // 2, 1)
    rot = jnp.where((lane % SA_DH) < SA_DH // 2, -bwd, fwd)
    return x * cos2 + rot * sin2


def _dsa_kernel(q_ref, qi_ref, smq_ref, k_ref, smk_ref, vt_ref, cos_ref, sin_ref, o_ref,
                kr, kir, score_s, sel_s, *, front, k_top):
    lb = k_ref.shape[0]
    n_kt = lb // LANE
    qb = pl.program_id(1)

    @pl.when(qb == 0)
    def _():
        def body(t, carry):
            sl = pl.ds(pl.multiple_of(t * LANE, LANE), LANE)
            cos2, sin2 = cos_ref[sl, :], sin_ref[sl, :]
            for j in range(SA_HEADS // 2):
                r = _rope(k_ref[sl, j * LANE:(j + 1) * LANE], cos2, sin2).astype(BF16)
                kr[2 * j, sl, :] = r[:, :SA_DH]
                kr[2 * j + 1, sl, :] = r[:, SA_DH:]
            kir[sl, :] = _rope(smk_ref[sl, :], cos2, sin2)[:, :IDX_DH].astype(BF16)
            return carry
        lax.fori_loop(0, n_kt, body, 0)

    qsl = pl.ds(pl.multiple_of(qb * QBLK, QBLK), QBLK)
    cos2, sin2 = cos_ref[qsl, :], sin_ref[qsl, :]
    qr = [(_rope(q_ref[:, j * LANE:(j + 1) * LANE], cos2, sin2) * (SA_DH ** -0.5)).astype(BF16)
          for j in range(SA_HEADS // 2)]
    qir = [_rope(qi_ref[:, j * LANE:(j + 1) * LANE], cos2, sin2).astype(BF16)
           for j in range(IDX_HEADS // 2)]
    w_t = smq_ref[...].T * (IDX_HEADS ** -0.5 * IDX_DH ** -0.5)

    key_row = lax.broadcasted_iota(jnp.int32, (lb, QBLK), 0)
    q_row = qb * QBLK + lax.broadcasted_iota(jnp.int32, (lb, QBLK), 1)
    causal = (key_row <= q_row) & (key_row >= front)

    score = jnp.zeros((lb, QBLK), F32)
    kib = kir[...]
    for h in range(IDX_HEADS):
        qh = qir[h // 2][:, (h % 2) * IDX_DH:(h % 2 + 1) * IDX_DH]
        rel = _nt(kib, qh)
        score = score + jnp.maximum(rel, 0.0) * w_t[SM_IDW + h:SM_IDW + h + 1, :]
    score_s[...] = jnp.where(causal, score, -jnp.inf)

    def key_to_float(c):
        c = jnp.maximum(c, jnp.int32(-2 ** 31 + 0x7FFFFF))
        return pltpu.bitcast(jnp.where(c < 0, c ^ jnp.int32(0x7FFFFFFF), c), F32)

    def count_ge(c):
        return jnp.sum(jnp.where(score_s[...] >= key_to_float(c), 1, 0), axis=0, keepdims=True)

    t0 = jnp.where(count_ge(jnp.zeros((1, QBLK), jnp.int32)) >= k_top,
                   jnp.int32(0), jnp.int32(-2 ** 31))

    def bit_step(i, t):
        cand = t | jnp.left_shift(jnp.int32(1), 30 - i)
        return jnp.where(count_ge(cand) >= k_top, cand, t)

    thr = key_to_float(lax.fori_loop(0, 31, bit_step, t0))
    score = score_s[...]
    gt = score > thr
    need = k_top - jnp.sum(jnp.where(gt, 1, 0), axis=0, keepdims=True)
    eq = score == thr
    ri = lax.broadcasted_iota(jnp.int32, (LANE, LANE), 0)
    ci = lax.broadcasted_iota(jnp.int32, (LANE, LANE), 1)
    tril_b = (ri >= ci).astype(BF16)
    carry = jnp.zeros((1, QBLK), F32)
    for t in range(n_kt):
        sl = slice(t * LANE, (t + 1) * LANE)
        eq_t = eq[sl, :]
        rank = _mm(tril_b, eq_t.astype(BF16)) + carry
        carry = rank[LANE - 1:LANE, :]
        sel_t = (gt[sl, :] | (eq_t & (rank <= need.astype(F32)))) & causal[sl, :]
        sel_s[sl, :] = jnp.where(sel_t, 0.0, -jnp.inf)
    mask_add = sel_s[...]

    for j in range(SA_HEADS // 2):
        halves = []
        for hh in range(2):
            h = 2 * j + hh
            qh = qr[j][:, hh * SA_DH:(hh + 1) * SA_DH]
            lg = _nt(kr[h], qh) + mask_add
            m = jnp.max(lg, axis=0, keepdims=True)
            m = jnp.where(m == -jnp.inf, 0.0, m)
            p = jnp.exp(lg - m)
            den = jnp.sum(p, axis=0, keepdims=True)
            o_t = _mm(vt_ref[h * SA_DH:(h + 1) * SA_DH, :], p.astype(BF16))
            halves.append(o_t / jnp.where(den == 0.0, 1.0, den))
        o_ref[:, j * LANE:(j + 1) * LANE] = jnp.concatenate(halves, axis=0).T.astype(o_ref.dtype)


def _dsa(proj, vt, cos2, sin2, bsz, lb, front, k_top):
    n_qb = lb // QBLK
    sa_w = SA_HEADS * SA_DH
    return pl.pallas_call(
        functools.partial(_dsa_kernel, front=front, k_top=k_top),
        out_shape=jax.ShapeDtypeStruct((bsz * lb, sa_w), BF16),
        grid=(bsz, n_qb),
        in_specs=[pl.BlockSpec((QBLK, sa_w), lambda b, i: (b * n_qb + i, C_SAQ // sa_w)),
                  pl.BlockSpec((QBLK, sa_w), lambda b, i: (b * n_qb + i, C_IDQ // sa_w)),
                  pl.BlockSpec((QBLK, LANE), lambda b, i: (b * n_qb + i, C_SMALL // LANE)),
                  pl.BlockSpec((lb, sa_w), lambda b, i: (b, C_SAK // sa_w)),
                  pl.BlockSpec((lb, LANE), lambda b, i: (b, C_SMALL // LANE)),
                  pl.BlockSpec((None, sa_w, lb), lambda b, i: (b, 0, 0)),
                  pl.BlockSpec((lb, LANE), lambda b, i: (0, 0)),
                  pl.BlockSpec((lb, LANE), lambda b, i: (0, 0))],
        out_specs=pl.BlockSpec((QBLK, sa_w), lambda b, i: (b * n_qb + i, 0)),
        scratch_shapes=[pltpu.VMEM((SA_HEADS, lb, SA_DH), BF16),
                        pltpu.VMEM((lb, IDX_DH), BF16),
                        pltpu.VMEM((lb, QBLK), F32),
                        pltpu.VMEM((lb, QBLK), F32)],
        compiler_params=_params("parallel", "arbitrary"),
        name="dsa",
    )(proj, proj, proj, proj, proj, vt, cos2, sin2)


def _layer_norm(y, g, b):
    mu = jnp.mean(y, axis=-1, keepdims=True)
    yc = y - mu
    var = jnp.mean(yc * yc, axis=-1, keepdims=True)
    return yc * lax.rsqrt(var + 1e-5) * g + b


def _out_kernel(odn_ref, osa_ref, h_ref, wa_ref, wb_ref, g_ref, b_ref, o_ref, ob_ref):
    mix = _mm(odn_ref[...], wa_ref[...]) + _mm(osa_ref[...], wb_ref[...])
    out = _layer_norm(DEEPNORM_ALPHA * h_ref[...] + mix, g_ref[...], b_ref[...])
    o_ref[...] = out
    ob_ref[...] = out.astype(BF16)


def _out_proj_ln(o_dn, o_sa, h, w_a, w_b, g, b, tm):
    m, d = h.shape
    row = lambda w: pl.BlockSpec((tm, w), lambda i: (i, 0))
    full = lambda a: pl.BlockSpec(a.shape, lambda i: (0, 0))
    return pl.pallas_call(
        _out_kernel,
        out_shape=(jax.ShapeDtypeStruct((m, d), F32), jax.ShapeDtypeStruct((m, d), BF16)),
        grid=(m // tm,),
        in_specs=[row(o_dn.shape[1]), row(o_sa.shape[1]), row(d), full(w_a), full(w_b), full(g), full(b)],
        out_specs=(row(d), row(d)),
        compiler_params=_params("parallel"),
        name="out_proj_ln",
    )(o_dn, o_sa, h, w_a, w_b, g, b)


R_EXP0 = N_GROUPS


def _route(logits):
    lane = lax.broadcasted_iota(jnp.int32, logits.shape, 1)
    ninf = -jnp.inf
    gl = jnp.where(lane < N_GROUPS, logits, ninf)
    gmax = jnp.max(gl, axis=-1, keepdims=True)
    gsel = jnp.min(jnp.where(gl == gmax, lane, LANE), axis=-1, keepdims=True)
    g_prob = 1.0 / jnp.sum(jnp.exp(gl - gmax), axis=-1, keepdims=True)
    lo = R_EXP0 + gsel * EXP_PER_GROUP
    el = jnp.where((lane >= lo) & (lane < lo + EXP_PER_GROUP), logits, ninf)
    m1 = jnp.max(el, axis=-1, keepdims=True)
    i1 = jnp.min(jnp.where(el == m1, lane, LANE), axis=-1, keepdims=True)
    el2 = jnp.where(lane == i1, ninf, el)
    m2 = jnp.max(el2, axis=-1, keepdims=True)
    i2 = jnp.min(jnp.where(el2 == m2, lane, LANE), axis=-1, keepdims=True)
    z = jnp.sum(jnp.exp(el - m1), axis=-1, keepdims=True)
    p1 = 1.0 / z
    p2 = jnp.exp(m2 - m1) / z
    gate1 = g_prob * p1 / (p1 + p2)
    gate2 = g_prob * p2 / (p1 + p2)
    return jnp.where(lane == i1, gate1, 0.0) + jnp.where(lane == i2, gate2, 0.0)


def _moe_kernel(h_ref, hb_ref, wr_ref, br_ref, w1_ref, w3_ref, w2_ref, g_ref, b_ref,
                o_ref, ob_ref, acc, comb):
    e = pl.program_id(1)

    @pl.when(e == 0)
    def _():
        logits = _mm(h_ref[...], wr_ref[...], precision=HIGHEST) + br_ref[...]
        comb[...] = _route(logits)
        acc[...] = jnp.zeros_like(acc)

    lane = lax.broadcasted_iota(jnp.int32, comb.shape, 1)
    ce = jnp.sum(jnp.where(lane == R_EXP0 + e, comb[...], 0.0), axis=-1, keepdims=True)
    xb = hb_ref[...]
    hid = _silu(_mm(xb, w1_ref[...])) * _mm(xb, w3_ref[...]) * ce
    acc[...] += _mm(hid.astype(BF16), w2_ref[...])

    @pl.when(e == N_EXPERTS - 1)
    def _():
        out = _layer_norm(DEEPNORM_ALPHA * h_ref[...] + acc[...], g_ref[...], b_ref[...])
        o_ref[...] = out
        ob_ref[...] = out.astype(BF16)


def _moe_ln(h, hb, w_r, b_r, w1, w3, w2, g, b, tm):
    m, d = h.shape
    row = lambda: pl.BlockSpec((tm, d), lambda i, e: (i, 0))
    full = lambda a: pl.BlockSpec(a.shape, lambda i, e: (0, 0))
    return pl.pallas_call(
        _moe_kernel,
        out_shape=(jax.ShapeDtypeStruct((m, d), F32), jax.ShapeDtypeStruct((m, d), BF16)),
        grid=(m // tm, N_EXPERTS),
        in_specs=[row(), row(), full(w_r), full(b_r),
                  pl.BlockSpec((None, d, D_EXPERT), lambda i, e: (e, 0, 0)),
                  pl.BlockSpec((None, d, D_EXPERT), lambda i, e: (e, 0, 0)),
                  pl.BlockSpec((None, D_EXPERT, d), lambda i, e: (e, 0, 0)),
                  full(g), full(b)],
        out_specs=(row(), row()),
        scratch_shapes=[pltpu.VMEM((tm, d), F32), pltpu.VMEM((tm, LANE), F32)],
        compiler_params=_params("parallel", "arbitrary"),
        name="moe_ln",
    )(h, hb, w_r, b_r, w1, w3, w2, g, b)


def _rope_tables(lb, front):
    pos = jnp.maximum(jnp.arange(lb, dtype=jnp.int32) - front, 0).astype(F32)
    inv = 1.0 / (ROPE_THETA ** (jnp.arange(0, SA_DH, 2, dtype=F32) / SA_DH))
    ang = pos[:, None] * inv[None, :]
    ang = jnp.concatenate([ang, ang, ang, ang], -1)
    return jnp.cos(ang), jnp.sin(ang)


def kernel(x, meta_tokens, w_in, conv_w, a_log, dt_bias, dn_norm_g, w_out, ln1_g, ln1_b,
           w_grp, b_grp, w_rtr, b_rtr, w1, w3, w2, ln2_g, ln2_b):
    bsz, seq, d = x.shape
    total = seq + N_META
    front = (-N_META) % CHUNK
    lb = -(-(front + total) // LANE) * LANE
    k_top = min(K_MAX, seq // 4)
    tm = lb // 2

    meta = jnp.broadcast_to(meta_tokens[None].astype(x.dtype), (bsz, N_META, d))
    h = jnp.concatenate([jnp.zeros((bsz, front, d), x.dtype), meta, x,
                         jnp.zeros((bsz, lb - front - total, d), x.dtype)], axis=1).reshape(bsz * lb, d)
    hb = h.astype(BF16)
    cos2, sin2 = _rope_tables(lb, front)

    dk, dv, sw = DN_HEADS * DN_DK, DN_HEADS * DN_DV, SA_HEADS * SA_DH
    o = np.cumsum([0, dk, dk, dv, dv, DN_HEADS, DN_HEADS, sw, sw, sw, IDX_HEADS * IDX_DH, IDX_DH, IDX_HEADS])
    col = lambda w, i: w[:, o[i]:o[i + 1]]

    for l in range(DEPTH):
        wl = w_in[l]
        small = jnp.concatenate([col(wl, 10), col(wl, 4), col(wl, 5), col(wl, 11)], axis=1)
        small = jnp.pad(small, ((0, 0), (0, LANE - small.shape[1])))
        w_perm = jnp.concatenate([col(wl, 0), col(wl, 1), col(wl, 2), col(wl, 3),
                                  col(wl, 6), col(wl, 7), col(wl, 9), small], axis=1).astype(BF16)
        proj = _in_proj(hb, w_perm, lb // 4)
        vt = _v_transposed(hb, col(wl, 8).T.astype(BF16), bsz, lb)
        hp = jnp.zeros((8, LANE), F32).at[0, :DN_HEADS].set(a_log[l]).at[1, :DN_HEADS].set(dt_bias[l])
        o_dn = _deltanet(proj, hp, conv_w[l], dn_norm_g[l][None, :], bsz, lb, front, total)
        o_sa = _dsa(proj, vt, cos2, sin2, bsz, lb, front, k_top)
        wo = w_out[l].astype(BF16)
        h, hb = _out_proj_ln(o_dn, o_sa, h, wo[:dv], wo[dv:], ln1_g[l][None, :], ln1_b[l][None, :], tm)
        w_r = jnp.pad(jnp.concatenate([w_grp[l], w_rtr[l]], axis=1), ((0, 0), (0, LANE - N_GROUPS - N_EXPERTS)))
        b_r = jnp.pad(jnp.concatenate([b_grp[l], b_rtr[l]]), (0, LANE - N_GROUPS - N_EXPERTS))[None, :]
        h, hb = _moe_ln(h, hb, w_r, b_r, w1[l].astype(BF16), w3[l].astype(BF16), w2[l].astype(BF16),
                        ln2_g[l][None, :], ln2_b[l][None, :], tm)
    return h.reshape(bsz, lb, d)[:, front + N_META:front + total]
```
